```python
import math
import jax, jax.numpy as jnp
from jax import lax
import numpy as np

D_MODEL = 1024
BATCH = 16
SEQ = 4096
DEPTH = 4
DEC_BATCH = 32
DEC_SEQ = 16
PAST_LEN = 4096

CHUNK = 64
QBLOCK = 128
N_BRANCH = 4
BRANCH_W = D_MODEL // N_BRANCH
CONV_W = BRANCH_W
CONV_K = 3
DIFF_HEADS = 4
DIFF_DH = BRANCH_W // (2 * DIFF_HEADS)
DIFF_VD = 2 * DIFF_DH
DIFF_QK = DIFF_HEADS * 2 * DIFF_DH
DIFF_V = DIFF_HEADS * DIFF_VD
FOX_HEADS = 4
FOX_DH = BRANCH_W // FOX_HEADS
FOX_QKV = FOX_HEADS * FOX_DH
POOL_WINDOWS = (2, 4, 8, 16)
POOL_GROUPS = 4
POOL_W = BRANCH_W
POOL_GC = POOL_W // POOL_GROUPS
POOL_HIST = 15
D_FF = ((8 * D_MODEL + 3 * 256 - 1) // (3 * 256)) * 256

IN_SPLITS = (CONV_W, CONV_W, CONV_W, DIFF_QK, DIFF_QK, DIFF_V, FOX_QKV, FOX_QKV, FOX_QKV, FOX_HEADS, POOL_W)
N_IN = 3 * CONV_W + 2 * DIFF_QK + DIFF_V + 3 * FOX_QKV + FOX_HEADS + POOL_W
NEG_INF = -1e30

kernel_name = 'hybrid_streaming_encoder_step'


def _rmsnorm(x, g, eps=1e-6):
    xf = x.astype(jnp.float32)
    y = xf * lax.rsqrt(jnp.mean(xf * xf, axis=-1, keepdims=True) + eps)
    return (y * g.astype(jnp.float32)).astype(x.dtype)


def _split_cols(y):
    outs, o = [], 0
    for n in IN_SPLITS:
        outs.append(y[..., o:o + n])
        o += n
    return outs


def _alibi_slopes():
    return 2.0 ** (-8.0 * (jnp.arange(DIFF_HEADS, dtype=jnp.float32) + 1.0) / DIFF_HEADS)


def _short_conv(z_ext, w):
    L = z_ext.shape[1] - (CONV_K - 1)
    return sum(w[i] * z_ext[:, i:i + L] for i in range(CONV_K))


def _pool_mix(u_ext, start_pos, w_grp, scale):
    B, T, _ = u_ext.shape
    L = T - POOL_HIST
    cs = jnp.cumsum(u_ext.astype(jnp.float32), axis=1)
    cs = jnp.concatenate([jnp.zeros((B, 1, POOL_W), jnp.float32), cs], axis=1)
    pos = start_pos + jnp.arange(L)
    u = u_ext[:, POOL_HIST:]
    outs = []
    for g, w in enumerate(POOL_WINDOWS):
        lo, hi = g * POOL_GC, (g + 1) * POOL_GC
        wsum = cs[:, POOL_HIST + 1:, lo:hi] - cs[:, POOL_HIST + 1 - w:POOL_HIST + 1 - w + L, lo:hi]
        cnt = jnp.minimum(w, pos + 1).astype(jnp.float32)[None, :, None]
        outs.append(wsum / cnt)
    d = (jnp.concatenate(outs, axis=-1) - u.astype(jnp.float32)).astype(u.dtype)
    d = d.reshape(B, L, POOL_GROUPS, POOL_GC)
    y = jnp.einsum('blgc,gce->blge', d, w_grp).reshape(B, L, POOL_W)
    return y * scale


def _diff_attn_core(q, k, v, qpos, kpos, lam):
    s = jnp.einsum('bqhcd,bkhcd->bhcqk', q, k).astype(jnp.float32) * (DIFF_DH ** -0.5)
    dist = jnp.abs(qpos[:, None] - kpos[None, :]).astype(jnp.float32)
    allowed = (kpos[None, :] // CHUNK) <= (qpos[:, None] // CHUNK)
    bias = jnp.where(allowed, -_alibi_slopes()[:, None, None] * dist, NEG_INF)
    p = jax.nn.softmax(s + bias[None, :, None], axis=-1)
    a = p[:, :, 0] - lam * p[:, :, 1]
    return jnp.einsum('bhqk,bkhe->bqhe', a.astype(v.dtype), v)


def _diff_attn_prompt(q, k, v, lam):
    B, S = q.shape[:2]
    nb = S // QBLOCK
    qb = jnp.moveaxis(q.reshape(B, nb, QBLOCK, DIFF_HEADS, 2, DIFF_DH), 1, 0)
    kpos = jnp.arange(S)

    def blk(args):
        qi, j = args
        return _diff_attn_core(qi, k, v, j * QBLOCK + jnp.arange(QBLOCK), kpos, lam)

    o = lax.map(blk, (qb, jnp.arange(nb)))
    return jnp.moveaxis(o, 0, 1).reshape(B, S, DIFF_HEADS, DIFF_VD)


def _fox_core(q, k, v, fq, fk, qpos, kpos):
    s = jnp.einsum('bqhd,bkhd->bhqk', q, k).astype(jnp.float32) * (FOX_DH ** -0.5)
    decay = jnp.swapaxes(fq, 1, 2)[..., :, None] - jnp.swapaxes(fk, 1, 2)[..., None, :]
    causal = kpos[None, :] <= qpos[:, None]
    p = jax.nn.softmax(jnp.where(causal, s + decay, NEG_INF), axis=-1)
    return jnp.einsum('bhqk,bkhd->bqhd', p.astype(v.dtype), v)


def _fox_prompt(q, k, v, F):
    B, S = q.shape[:2]
    nb = S // QBLOCK
    qb = jnp.moveaxis(q.reshape(B, nb, QBLOCK, FOX_HEADS, FOX_DH), 1, 0)
    Fb = jnp.moveaxis(F.reshape(B, nb, QBLOCK, FOX_HEADS), 1, 0)
    kpos = jnp.arange(S)

    def blk(args):
        qi, fi, j = args
        return _fox_core(qi, k, v, fi, F, j * QBLOCK + jnp.arange(QBLOCK), kpos)

    o = lax.map(blk, (qb, Fb, jnp.arange(nb)))
    return jnp.moveaxis(o, 0, 1).reshape(B, S, FOX_HEADS, FOX_DH)


def _layer(x, l, start_pos, hist, W):
    (g_norm, w_in, b_forget, conv_w, lambda_qk, diff_subln, pool_w, pool_scale,
     w_branch, w_gate, b_gate, w_out, w_ffn_in, w_ffn_out) = W
    B, L, _ = x.shape
    h = _rmsnorm(x, g_norm[l, 0])
    a_x, a_b, a_c, dq, dk, dv, fq, fk, fv, ff, pu = _split_cols(h @ w_in[l])

    z = a_c * a_x
    z_hist = jnp.zeros((B, CONV_K - 1, CONV_W), z.dtype) if hist is None else hist[5]
    z_ext = jnp.concatenate([z_hist, z], axis=1)
    o_a = a_b * _short_conv(z_ext, conv_w[l])
    new_conv = z_ext[:, -(CONV_K - 1):]

    dq = dq.reshape(B, L, DIFF_HEADS, 2, DIFF_DH)
    dk = dk.reshape(B, L, DIFF_HEADS, 2, DIFF_DH)
    dv = dv.reshape(B, L, DIFF_HEADS, DIFF_VD)
    lam_init = 0.8 - 0.6 * math.exp(-0.3 * l)
    lqk = lambda_qk[l].astype(jnp.float32)
    lam = jnp.exp(jnp.sum(lqk[0] * lqk[1])) - jnp.exp(jnp.sum(lqk[2] * lqk[3])) + lam_init
    if hist is None:
        o = _diff_attn_prompt(dq, dk, dv, lam)
    else:
        P = hist[0].shape[1]
        kk = jnp.concatenate([hist[0], dk], axis=1)
        vv = jnp.concatenate([hist[1], dv], axis=1)
        o = _diff_attn_core(dq, kk, vv, P + jnp.arange(L), jnp.arange(P + L), lam)
    o_b = (_rmsnorm(o, diff_subln[l]) * (1.0 - lam_init)).reshape(B, L, BRANCH_W)

    fq = fq.reshape(B, L, FOX_HEADS, FOX_DH)
    fk = fk.reshape(B, L, FOX_HEADS, FOX_DH)
    fv = fv.reshape(B, L, FOX_HEADS, FOX_DH)
    lf = jax.nn.log_sigmoid(ff.astype(jnp.float32) + b_forget[l].astype(jnp.float32))
    if hist is None:
        F = jnp.cumsum(lf, axis=1)
        oc = _fox_prompt(fq, fk, fv, F)
    else:
        P = hist[2].shape[1]
        F_all = jnp.cumsum(jnp.concatenate([hist[4].astype(jnp.float32), lf], axis=1), axis=1)
        kk = jnp.concatenate([hist[2], fk], axis=1)
        vv = jnp.concatenate([hist[3], fv], axis=1)
        oc = _fox_core(fq, kk, vv, F_all[:, P:], F_all, P + jnp.arange(L), jnp.arange(P + L))
    o_c = oc.reshape(B, L, BRANCH_W)

    u_hist = jnp.zeros((B, POOL_HIST, POOL_W), pu.dtype) if hist is None else hist[6]
    u_ext = jnp.concatenate([u_hist, pu], axis=1)
    o_d = _pool_mix(u_ext, start_pos, pool_w[l], pool_scale[l])
    new_pool = u_ext[:, -POOL_HIST:]

    merged = None
    for i, o_i in enumerate((o_a, o_b, o_c, o_d)):
        gate = jax.nn.sigmoid((h @ w_gate[l, i] + b_gate[l, i]).astype(jnp.float32)).astype(h.dtype)
        term = gate * (o_i @ w_branch[l, i])
        merged = term if merged is None else merged + term
    x = x + _rmsnorm(merged @ w_out[l], g_norm[l, 1])

    h2 = _rmsnorm(x, g_norm[l, 2])
    gu = h2 @ w_ffn_in[l]
    f = (jax.nn.silu(gu[..., :D_FF]) * gu[..., D_FF:]) @ w_ffn_out[l]
    x = x + _rmsnorm(f, g_norm[l, 3])
    return x, (dk, dv, fk, fv, lf, new_conv, new_pool)


def setup_inputs(seed: int = 0) -> dict:
    key = jax.random.key(seed)
    ks = jax.random.split(key, 24)
    nrm = jax.random.normal
    f32 = jnp.float32
    return {
        'x_prompt': nrm(ks[0], (BATCH, SEQ, D_MODEL), f32),
        'x_sample': nrm(ks[1], (DEC_BATCH, DEC_SEQ, D_MODEL), f32),
        'cache_diff_k': nrm(ks[2], (DEPTH, DEC_BATCH, PAST_LEN, DIFF_HEADS, 2, DIFF_DH), f32),
        'cache_diff_v': nrm(ks[3], (DEPTH, DEC_BATCH, PAST_LEN, DIFF_HEADS, DIFF_VD), f32),
        'cache_fox_k': nrm(ks[4], (DEPTH, DEC_BATCH, PAST_LEN, FOX_HEADS, FOX_DH), f32),
        'cache_fox_v': nrm(ks[5], (DEPTH, DEC_BATCH, PAST_LEN, FOX_HEADS, FOX_DH), f32),
        'cache_fox_lf': jax.nn.log_sigmoid(3.0 + nrm(ks[6], (DEPTH, DEC_BATCH, PAST_LEN, FOX_HEADS), f32)),
        'state_conv': nrm(ks[7], (DEPTH, DEC_BATCH, CONV_K - 1, CONV_W), f32),
        'state_pool': nrm(ks[8], (DEPTH, DEC_BATCH, POOL_HIST, POOL_W), f32),
        'g_norm': 1.0 + 0.05 * nrm(ks[9], (DEPTH, 4, D_MODEL), f32),
        'w_in': nrm(ks[10], (DEPTH, D_MODEL, N_IN), f32) * D_MODEL ** -0.5,
        'b_forget': jnp.linspace(1.0, 4.0, FOX_HEADS, dtype=f32)[None] + 0.1 * nrm(ks[11], (DEPTH, FOX_HEADS), f32),
        'conv_w': nrm(ks[12], (DEPTH, CONV_K, CONV_W), f32) * CONV_K ** -0.5,
        'lambda_qk': 0.1 * nrm(ks[13], (DEPTH, 4, DIFF_DH), f32),
        'diff_subln': 1.0 + 0.05 * nrm(ks[14], (DEPTH, DIFF_VD), f32),
        'pool_w': nrm(ks[15], (DEPTH, POOL_GROUPS, POOL_GC, POOL_GC), f32) * POOL_GC ** -0.5,
        'pool_scale': 1.0 + 0.05 * nrm(ks[16], (DEPTH, POOL_W), f32),
        'w_branch': nrm(ks[17], (DEPTH, N_BRANCH, BRANCH_W, D_MODEL), f32) * BRANCH_W ** -0.5,
        'w_gate': nrm(ks[18], (DEPTH, N_BRANCH, D_MODEL, D_MODEL), f32) * D_MODEL ** -0.5,
        'b_gate': 0.01 * nrm(ks[19], (DEPTH, N_BRANCH, D_MODEL), f32),
        'w_out': nrm(ks[20], (DEPTH, D_MODEL, D_MODEL), f32) * D_MODEL ** -0.5,
        'w_ffn_in': nrm(ks[21], (DEPTH, D_MODEL, 2 * D_FF), f32) * D_MODEL ** -0.5,
        'w_ffn_out': nrm(ks[22], (DEPTH, D_FF, D_MODEL), f32) * D_FF ** -0.5,
    }


def reference(x_prompt, x_sample, cache_diff_k, cache_diff_v, cache_fox_k, cache_fox_v, cache_fox_lf,
              state_conv, state_pool, g_norm, w_in, b_forget, conv_w, lambda_qk, diff_subln, pool_w,
              pool_scale, w_branch, w_gate, b_gate, w_out, w_ffn_in, w_ffn_out):
    W = (g_norm, w_in, b_forget, conv_w, lambda_qk, diff_subln, pool_w, pool_scale,
         w_branch, w_gate, b_gate, w_out, w_ffn_in, w_ffn_out)
    past = cache_diff_k.shape[2]
    yp, ys = x_prompt, x_sample
    p_states, s_states = [], []
    for l in range(DEPTH):
        yp, st = _layer(yp, l, 0, None, W)
        p_states.append(st)
        hist = (cache_diff_k[l], cache_diff_v[l], cache_fox_k[l], cache_fox_v[l], cache_fox_lf[l],
                state_conv[l], state_pool[l])
        ys, st = _layer(ys, l, past, hist, W)
        s_states.append(st)
    p_diff_k, p_diff_v, p_fox_k, p_fox_v, p_fox_lf, p_conv, p_pool = [jnp.stack(t) for t in zip(*p_states)]
    s_diff_k, s_diff_v, s_fox_k, s_fox_v, s_fox_lf, s_conv, s_pool = [jnp.stack(t) for t in zip(*s_states)]
    return (yp, ys, p_diff_k, p_diff_v, p_fox_k, p_fox_v, p_fox_lf, p_conv, p_pool,
            s_diff_k, s_diff_v, s_fox_k, s_fox_v, s_fox_lf, s_conv, s_pool)
```

```python
import functools
import math

import jax
import jax.numpy as jnp
from jax import lax
from jax.experimental import pallas as pl
from jax.experimental.pallas import tpu as pltpu

F32 = jnp.float32
BF16 = jnp.bfloat16

D_MODEL = 1024
BRANCH_W = 256
CHUNK = 64
CONV_K = 3
DIFF_HEADS = 4
DIFF_DH = 32
FOX_HEADS = 4
FOX_DH = 64
POOL_WINDOWS = (2, 4, 8, 16)
POOL_HIST = 15
D_FF = 2816
NEG = -1e30
LOG2E = 1.4426950408889634
EPS = 1e-6

N_MAIN = 9 * BRANCH_W
COL_PU = N_MAIN
COL_FF = N_MAIN + BRANCH_W
N_PACK = COL_FF + 128

LANE = 128
VMEM_LIMIT = 56 * 1024 * 1024


def _tiles(seq):
    tm = min(512, seq)
    tb = min(256, seq)
    return tm, tb


def _rms(x, g):
    return x * lax.rsqrt(jnp.mean(x * x, axis=-1, keepdims=True) + EPS) * g


def _dot(a, b):
    return jnp.dot(a, b, preferred_element_type=F32)


def _split3(x):
    hi = x.astype(BF16)
    r = x - hi.astype(F32)
    mid = r.astype(BF16)
    lo = (r - mid.astype(F32)).astype(BF16)
    return hi, mid, lo


def _params(sem):
    return pltpu.CompilerParams(dimension_semantics=sem, vmem_limit_bytes=VMEM_LIMIT)


def _whole(space=pltpu.VMEM):
    return pl.BlockSpec(memory_space=space)


def _inproj_kernel(prompt, st, tm, tb, start_pos, *refs):
    if prompt:
        (x_ref, g_ref, w_ref, cw_ref, bf_ref, pw_ref, ps_ref,
         oa_ref, od_ref, dq_ref, dk_ref, dv_ref, fq_ref, fk_ref, fv_ref, lf_ref,
         dkt_ref, fkt_ref, dvb_ref, fvb_ref, lft_ref, nc_ref, np_ref,
         zext, a0, a1, a2, a3) = refs
    else:
        (x_ref, g_ref, w_ref, cw_ref, bf_ref, pw_ref, ps_ref, zh_ref, uh_ref,
         oa_ref, od_ref, dq_ref, dk_ref, dv_ref, fq_ref, fk_ref, fv_ref, lf_ref,
         z_ref, u_ref,
         zext, a0, a1, a2, a3) = refs
    t_idx = pl.program_id(1)
    h = _rms(x_ref[0], g_ref[...]).astype(BF16)

    def proj(k):
        return _dot(h, w_ref[:, k * BRANCH_W:(k + 1) * BRANCH_W])

    pc = 8 if prompt else (CONV_K - 1) * st
    z = proj(2) * proj(0)
    if prompt:
        @pl.when(t_idx == 0)
        def _():
            zext[0:pc] = jnp.zeros((pc, BRANCH_W), F32)

        @pl.when(t_idx > 0)
        def _():
            zext[0:pc] = zext[tm:tm + pc]
    else:
        zext[0:pc] = zh_ref[...]
    zext[pc:pc + tm] = z
    cw = cw_ref[...]
    conv = cw[0:1] * zext[pc - 2 * st:pc - 2 * st + tm] + cw[1:2] * zext[pc - st:pc - st + tm] + cw[2:3] * z
    oa_ref[0] = (proj(1) * conv).astype(BF16)

    dq_ref[0] = proj(3).astype(BF16)
    dk = proj(4)
    dv = proj(5)
    dk_ref[0] = dk
    dv_ref[0] = dv
    fq_ref[0] = proj(6).astype(BF16)
    fk = proj(7)
    fv = proj(8)
    fk_ref[0] = fk
    fv_ref[0] = fv
    ff = _dot(h, w_ref[:, COL_FF:COL_FF + LANE]) + bf_ref[...]
    lf = jnp.minimum(ff, 0.0) - jnp.log1p(jnp.exp(-jnp.abs(ff)))
    lf_ref[0] = lf[:, 0:FOX_HEADS]
    if prompt:
        dvb_ref[0] = dv.astype(BF16)
        fvb_ref[0] = fv.astype(BF16)
        dkt = dk.T.astype(BF16)
        fkt = fk.T.astype(BF16)
        for i in range(tm // tb):
            dkt_ref[0, i] = dkt[:, i * tb:(i + 1) * tb]
            fkt_ref[0, i] = fkt[:, i * tb:(i + 1) * tb]
        lft_ref[0] = lf.T[0:8, :]

    u = _dot(h, w_ref[:, COL_PU:COL_PU + BRANCH_W])
    if prompt:
        p, lows = 32, (8, 16, 24, 32)
    else:
        p, lows = POOL_HIST * st, (st, 3 * st, 7 * st, 15 * st)
    r = p + tm
    if prompt:
        @pl.when(t_idx == 0)
        def _():
            a0[0:p] = jnp.zeros((p, BRANCH_W), F32)

        @pl.when(t_idx > 0)
        def _():
            a0[p - 16:p] = a0[r - 16:r]
    else:
        a0[0:p] = uh_ref[...]
    a0[p:r] = u
    a1[lows[0]:r] = a0[lows[0]:r] + a0[lows[0] - st:r - st]
    a2[lows[1]:r] = a1[lows[1]:r] + a1[lows[1] - 2 * st:r - 2 * st]
    a3[lows[2]:r] = a2[lows[2]:r] + a2[lows[2] - 4 * st:r - 4 * st]
    w16 = a3[p:r] + a3[p - 8 * st:r - 8 * st]
    lane = lax.broadcasted_iota(jnp.int32, (tm, BRANCH_W), 1)
    g0, g1, g2 = lane < 64, lane < 128, lane < 192
    wsum = jnp.where(g0, a1[p:r], jnp.where(g1, a2[p:r], jnp.where(g2, a3[p:r], w16)))
    wlen = jnp.where(g0, 2, jnp.where(g1, 4, jnp.where(g2, 8, 16)))
    if start_pos >= POOL_WINDOWS[-1] - 1:
        cnt = wlen.astype(F32)
    else:
        row = lax.broadcasted_iota(jnp.int32, (tm, BRANCH_W), 0)
        cnt = jnp.minimum(wlen, start_pos + t_idx * tm + row + 1).astype(F32)
    dlt = wsum / cnt - u
    od_ref[0] = (_dot(dlt.astype(BF16), pw_ref[...]) * ps_ref[...]).astype(BF16)
    if prompt:
        nc_ref[0] = zext[pc + tm - (CONV_K - 1):pc + tm]
        np_ref[0] = a0[r - POOL_HIST:r]
    else:
        z_ref[0] = z
        u_ref[0] = u


def _inproj(x, g, w, cw, bfp, pw, ps, *, prompt, st=1, zh=None, uh=None, start_pos=0):
    nb, seq, _ = x.shape
    tm, tb = _tiles(seq)
    if not prompt:
        tm = seq
    nt = seq // tm
    assert seq % tm == 0 and tm % tb == 0 and tm >= 32

    def tok(width, dtype):
        return jax.ShapeDtypeStruct((nb, seq, width), dtype), pl.BlockSpec((1, tm, width), lambda b, t: (b, t, 0))

    outs = [tok(BRANCH_W, BF16), tok(BRANCH_W, BF16), tok(BRANCH_W, BF16), tok(BRANCH_W, F32), tok(BRANCH_W, F32),
            tok(BRANCH_W, BF16), tok(BRANCH_W, F32), tok(BRANCH_W, F32), tok(FOX_HEADS, F32)]
    if prompt:
        kt = (jax.ShapeDtypeStruct((nb, seq // tb, BRANCH_W, tb), BF16),
              pl.BlockSpec((1, tm // tb, BRANCH_W, tb), lambda b, t: (b, t, 0, 0)))
        outs += [kt, kt, tok(BRANCH_W, BF16), tok(BRANCH_W, BF16),
                 (jax.ShapeDtypeStruct((nb, 8, seq), F32), pl.BlockSpec((1, 8, tm), lambda b, t: (b, 0, t))),
                 (jax.ShapeDtypeStruct((nb, CONV_K - 1, BRANCH_W), F32),
                  pl.BlockSpec((1, CONV_K - 1, BRANCH_W), lambda b, t: (b, 0, 0))),
                 (jax.ShapeDtypeStruct((nb, POOL_HIST, BRANCH_W), F32),
                  pl.BlockSpec((1, POOL_HIST, BRANCH_W), lambda b, t: (b, 0, 0)))]
        pc, p = 8, 32
    else:
        outs += [tok(BRANCH_W, F32), tok(BRANCH_W, F32)]
        pc, p = (CONV_K - 1) * st, POOL_HIST * st
    ins = [x, g, w, cw, bfp, pw, ps]
    in_specs = [pl.BlockSpec((1, tm, D_MODEL), lambda b, t: (b, t, 0))] + [_whole()] * 6
    if not prompt:
        ins += [zh, uh]
        in_specs += [_whole(), _whole()]
    scratch = [pltpu.VMEM((pc + tm, BRANCH_W), F32)] + [pltpu.VMEM((p + tm, BRANCH_W), F32)] * 4
    return pl.pallas_call(
        functools.partial(_inproj_kernel, prompt, st, tm, tb, start_pos),
        grid=(nb, nt),
        in_specs=in_specs,
        out_specs=[o[1] for o in outs],
        out_shape=[o[0] for o in outs],
        scratch_shapes=scratch,
        compiler_params=_params(("arbitrary", "arbitrary")),
        name="inproj_prompt" if prompt else "inproj_sample",
    )(*ins)


def _cumsum_kernel(nc, x_ref, o_ref):
    x = x_ref[0]
    rows = x.shape[0]
    ri = lax.broadcasted_iota(jnp.int32, (LANE, LANE), 0)
    ci = lax.broadcasted_iota(jnp.int32, (LANE, LANE), 1)
    upper = (ri <= ci).astype(BF16)
    within = sum(_dot(part, upper) for part in _split3(x))
    tot = jnp.broadcast_to(within[:, LANE - 1:LANE], (rows, LANE))
    rr = lax.broadcasted_iota(jnp.int32, (rows, rows), 0)
    rc = lax.broadcasted_iota(jnp.int32, (rows, rows), 1)
    sh = int(math.log2(nc))
    before = (((rr >> sh) == (rc >> sh)) & (rc < rr)).astype(BF16)
    offs = sum(_dot(before, part) for part in _split3(tot))
    o_ref[0] = within + offs


def _cumsum(lft):
    nb, _, seq = lft.shape
    nc = seq // LANE
    assert nc & (nc - 1) == 0
    x = lft.reshape(nb, 8 * nc, LANE)
    out = pl.pallas_call(
        functools.partial(_cumsum_kernel, nc),
        grid=(nb,),
        in_specs=[pl.BlockSpec((1, 8 * nc, LANE), lambda b: (b, 0, 0))],
        out_specs=pl.BlockSpec((1, 8 * nc, LANE), lambda b: (b, 0, 0)),
        out_shape=jax.ShapeDtypeStruct((nb, 8 * nc, LANE), F32),
        compiler_params=_params(("arbitrary",)),
        name="lf_cumsum",
    )(x)
    return out.reshape(nb, 8, seq)


def _mask_stack(q, nj, qs_ref):
    tq = q.shape[0]
    width = BRANCH_W // nj
    sh = int(math.log2(width))
    lane = lax.broadcasted_iota(jnp.int32, q.shape, 1)
    for j in range(nj):
        qs_ref[j * tq:(j + 1) * tq, :] = jnp.where((lane >> sh) == j, q, jnp.zeros_like(q))


def _online(t, rows, m_ref, l_ref, al_ref, p_ref):
    m_old = m_ref[rows]
    m_new = jnp.maximum(m_old, jnp.max(t, axis=-1, keepdims=True))
    alpha = jnp.exp2(m_old - m_new)
    p = jnp.exp2(t - m_new)
    l_ref[rows] = alpha * l_ref[rows] + jnp.sum(p, axis=-1, keepdims=True)
    m_ref[rows] = m_new
    al_ref[rows] = alpha
    p_ref[rows, :] = p.astype(BF16)


def _pv(nrows, p_ref, al_ref, acc_ref, v_lo, v_hi):
    half = nrows // 2
    for rows, vv in ((slice(0, half), v_lo), (slice(half, nrows), v_hi)):
        acc_ref[rows] = al_ref[rows] * acc_ref[rows] + _dot(p_ref[rows, :], vv)


def _pair_merge(parts):
    lane = lax.broadcasted_iota(jnp.int32, parts[0].shape, 1)
    lo = lane < 64
    return jnp.concatenate([jnp.where(lo, parts[0], parts[1]), jnp.where(lo, parts[2], parts[3])], axis=-1)


def _diff_lambda(lqk_ref, lam_init):
    lq = lqk_ref[...]
    a = jnp.sum(lq[0:1] * lq[1:2], axis=-1, keepdims=True)
    b = jnp.sum(lq[2:3] * lq[3:4], axis=-1, keepdims=True)
    return jnp.exp(a) - jnp.exp(b) + lam_init


def _diff_finish(tq, lam, out_scale, sub_ref, l_ref, acc_ref):
    heads = []
    for hd in range(DIFF_HEADS):
        r1 = slice((2 * hd) * tq, (2 * hd + 1) * tq)
        r2 = slice((2 * hd + 1) * tq, (2 * hd + 2) * tq)
        heads.append(acc_ref[r1] / l_ref[r1] - lam * (acc_ref[r2] / l_ref[r2]))
    o = _pair_merge(heads)
    lane = lax.broadcasted_iota(jnp.int32, o.shape, 1)
    grp = lane >> 6
    o2 = o * o
    ms = jnp.zeros_like(o)
    for hd in range(DIFF_HEADS):
        sel = grp == hd
        ms = jnp.where(sel, jnp.sum(jnp.where(sel, o2, 0.0), axis=-1, keepdims=True) * (1.0 / 64.0), ms)
    return (o * lax.rsqrt(ms + EPS) * sub_ref[...] * out_scale).astype(BF16)


def _fox_finish(tq, l_ref, acc_ref):
    heads = []
    for hd in range(FOX_HEADS):
        r1 = slice(hd * tq, (hd + 1) * tq)
        heads.append(acc_ref[r1] / l_ref[r1])
    return _pair_merge(heads).astype(BF16)


def _alibi_slope(hd):
    return 2.0 ** (-8.0 * (hd + 1.0) / DIFF_HEADS)


def _attn_scratch(nrows, tk):
    return [pltpu.VMEM((nrows, BRANCH_W), BF16), pltpu.VMEM((nrows, 1), F32), pltpu.VMEM((nrows, 1), F32),
            pltpu.VMEM((nrows, 1), F32), pltpu.VMEM((nrows, LANE), F32), pltpu.VMEM((nrows, tk), BF16)]


def _attn_init(m_ref, l_ref, acc_ref):
    m_ref[...] = jnp.full(m_ref.shape, NEG, F32)
    l_ref[...] = jnp.zeros(l_ref.shape, F32)
    acc_ref[...] = jnp.zeros(acc_ref.shape, F32)


def _diff_attn_kernel(tb, c_ref, lqk_ref, sub_ref, q_ref, kt_ref, v_ref, o_ref,
                      qs_ref, m_ref, l_ref, al_ref, acc_ref, p_ref):
    nj = 2 * DIFF_HEADS
    qb = pl.program_id(1)
    _mask_stack(q_ref[0], nj, qs_ref)
    _attn_init(m_ref, l_ref, acc_ref)
    c = DIFF_DH ** -0.5 * LOG2E
    kcol = lax.broadcasted_iota(jnp.int32, (1, tb), 1).astype(F32)

    def block(kb, diag):
        s = _dot(qs_ref[...], kt_ref[0, kb])
        if diag:
            ii = lax.broadcasted_iota(jnp.int32, (tb, tb), 0)
            jj = lax.broadcasted_iota(jnp.int32, (tb, tb), 1)
            sh = int(math.log2(CHUNK))
            allowed = (jj >> sh) <= (ii >> sh)
            rel = jnp.where(jj <= ii, jj, 2 * ii - jj).astype(F32)
        else:
            krel = ((kb - qb) * tb).astype(F32) + kcol
        for j in range(nj):
            rows = slice(j * tb, (j + 1) * tb)
            sl = _alibi_slope(j // 2) * LOG2E
            if diag:
                t = jnp.where(allowed, s[rows] * c + rel * sl, NEG)
            else:
                t = s[rows] * c + krel * sl
            _online(t, rows, m_ref, l_ref, al_ref, p_ref)
        _pv(nj * tb, p_ref, al_ref, acc_ref, v_ref[0, kb, :, 0:LANE], v_ref[0, kb, :, LANE:2 * LANE])

    def body(kb, carry):
        block(kb, False)
        return carry

    lax.fori_loop(0, qb, body, 0)
    block(qb, True)
    lam = _diff_lambda(lqk_ref, c_ref[0])
    o_ref[0] = _diff_finish(tb, lam, c_ref[1], sub_ref, l_ref, acc_ref)


def _fox_attn_kernel(tb, q_ref, kt_ref, v_ref, f_ref, o_ref,
                     qs_ref, m_ref, l_ref, al_ref, acc_ref, p_ref):
    nj = FOX_HEADS
    qb = pl.program_id(1)
    _mask_stack(q_ref[0], nj, qs_ref)
    _attn_init(m_ref, l_ref, acc_ref)
    c = FOX_DH ** -0.5 * LOG2E

    def block(kb, diag):
        s = _dot(qs_ref[...], kt_ref[0, kb])
        if diag:
            ii = lax.broadcasted_iota(jnp.int32, (tb, tb), 0)
            jj = lax.broadcasted_iota(jnp.int32, (tb, tb), 1)
            causal = jj <= ii
        for j in range(nj):
            rows = slice(j * tb, (j + 1) * tb)
            t = s[rows] * c - f_ref[0, kb, j:j + 1, :] * LOG2E
            if diag:
                t = jnp.where(causal, t, NEG)
            _online(t, rows, m_ref, l_ref, al_ref, p_ref)
        _pv(nj * tb, p_ref, al_ref, acc_ref, v_ref[0, kb, :, 0:LANE], v_ref[0, kb, :, LANE:2 * LANE])

    def body(kb, carry):
        block(kb, False)
        return carry

    lax.fori_loop(0, qb, body, 0)
    block(qb, True)
    o_ref[0] = _fox_finish(tb, l_ref, acc_ref)


def _prompt_attn(kind, q, kt, vb, extra):
    nb, seq, _ = q.shape
    _, tb = _tiles(seq)
    nkb = seq // tb
    v4 = vb.reshape(nb, nkb, tb, BRANCH_W)
    qspec = pl.BlockSpec((1, tb, BRANCH_W), lambda b, i: (b, i, 0))
    ktspec = pl.BlockSpec((1, nkb, BRANCH_W, tb), lambda b, i: (b, 0, 0, 0))
    vspec = pl.BlockSpec((1, nkb, tb, BRANCH_W), lambda b, i: (b, 0, 0, 0))
    if kind == "diff":
        consts, lqk, sub = extra
        nj = 2 * DIFF_HEADS
        kern = functools.partial(_diff_attn_kernel, tb)
        ins = [consts, lqk, sub, q, kt, v4]
        in_specs = [_whole(pltpu.SMEM), _whole(), _whole(), qspec, ktspec, vspec]
    else:
        (fk,) = extra
        nj = FOX_HEADS
        kern = functools.partial(_fox_attn_kernel, tb)
        ins = [q, kt, v4, fk]
        in_specs = [qspec, ktspec, vspec, pl.BlockSpec((1, nkb, 8, tb), lambda b, i: (b, 0, 0, 0))]
    return pl.pallas_call(
        kern,
        grid=(nb, nkb),
        in_specs=in_specs,
        out_specs=qspec,
        out_shape=jax.ShapeDtypeStruct((nb, seq, BRANCH_W), BF16),
        scratch_shapes=_attn_scratch(nj * tb, tb),
        compiler_params=_params(("arbitrary", "arbitrary")),
        name=kind + "_attn_prompt",
    )(*ins)


def _nt_dot(a, b):
    return lax.dot_general(a, b, (((1,), (1,)), ((), ())), preferred_element_type=F32)


def _row_group_const(nrows, nt, values):
    r = lax.broadcasted_iota(jnp.int32, (nrows, 1), 0)
    out = jnp.full((nrows, 1), values[-1], F32)
    for g in range(len(values) - 2, -1, -1):
        out = jnp.where(r < (g + 1) * nt, values[g], out)
    return out


def _diff_dec_kernel(nt, tkc, past, c_ref, lqk_ref, sub_ref, q_ref, kn_ref, vn_ref, kc_ref, vc_ref, o_ref,
                     qs_ref, m_ref, l_ref, al_ref, acc_ref, p_ref, pn_ref):
    nj = 2 * DIFF_HEADS
    nrows = nj * nt
    ci = pl.program_id(1)
    c = DIFF_DH ** -0.5 * LOG2E
    slope = _row_group_const(nrows, nt, [_alibi_slope(j // 2) * LOG2E for j in range(nj)])
    rows = slice(0, nrows)

    @pl.when(ci == 0)
    def _():
        _mask_stack(q_ref[0], nj, qs_ref)
        _attn_init(m_ref, l_ref, acc_ref)

    kmat = kc_ref[0].astype(BF16)
    vmat = vc_ref[0].astype(BF16)
    krel = (ci * tkc - past).astype(F32) + lax.broadcasted_iota(jnp.int32, (1, tkc), 1).astype(F32)
    t = _nt_dot(qs_ref[...], kmat) * c + slope * krel
    _online(t, rows, m_ref, l_ref, al_ref, p_ref)
    _pv(nrows, p_ref, al_ref, acc_ref, vmat[:, 0:LANE], vmat[:, LANE:2 * LANE])

    @pl.when(ci == pl.num_programs(1) - 1)
    def _():
        kn = kn_ref[0].astype(BF16)
        vn = vn_ref[0].astype(BF16)
        tq = lax.broadcasted_iota(jnp.int32, (nrows, LANE), 0) & (nt - 1)
        tk = lax.broadcasted_iota(jnp.int32, (nrows, LANE), 1)
        sh = int(math.log2(CHUNK))
        allowed = (((past + tk) >> sh) <= ((past + tq) >> sh)) & (tk < nt)
        rel = jnp.where(tk <= tq, tk, 2 * tq - tk).astype(F32)
        tn = jnp.where(allowed, _nt_dot(qs_ref[...], kn) * c + slope * rel, NEG)
        _online(tn, rows, m_ref, l_ref, al_ref, pn_ref)
        _pv(nrows, pn_ref, al_ref, acc_ref, vn[:, 0:LANE], vn[:, LANE:2 * LANE])
        lam = _diff_lambda(lqk_ref, c_ref[0])
        o_ref[0] = _diff_finish(nt, lam, c_ref[1], sub_ref, l_ref, acc_ref)


def _fox_dec_kernel(nt, tkc, q_ref, kn_ref, vn_ref, lfn_ref, kc_ref, vc_ref, fc_ref, o_ref,
                    qs_ref, m_ref, l_ref, al_ref, acc_ref, p_ref, pn_ref):
    nj = FOX_HEADS
    nrows = nj * nt
    ci = pl.program_id(1)
    c = FOX_DH ** -0.5 * LOG2E
    rows = slice(0, nrows)

    def head_rows(f):
        return jnp.concatenate([jnp.broadcast_to(f[j:j + 1], (nt, f.shape[1])) for j in range(nj)], axis=0)

    @pl.when(ci == 0)
    def _():
        _mask_stack(q_ref[0], nj, qs_ref)
        _attn_init(m_ref, l_ref, acc_ref)

    kmat = kc_ref[0].astype(BF16)
    vmat = vc_ref[0].astype(BF16)
    fc = fc_ref[0]
    t = _nt_dot(qs_ref[...], kmat) * c - head_rows(fc) * LOG2E
    _online(t, rows, m_ref, l_ref, al_ref, p_ref)
    _pv(nrows, p_ref, al_ref, acc_ref, vmat[:, 0:LANE], vmat[:, LANE:2 * LANE])

    @pl.when(ci == pl.num_programs(1) - 1)
    def _():
        kn = kn_ref[0].astype(BF16)
        vn = vn_ref[0].astype(BF16)
        ri = lax.broadcasted_iota(jnp.int32, (LANE, LANE), 0)
        cj = lax.broadcasted_iota(jnp.int32, (LANE, LANE), 1)
        upper = (ri <= cj).astype(BF16)
        fnew = fc[:, tkc - 1:tkc] + sum(_dot(part, upper) for part in _split3(lfn_ref[0]))
        tq = lax.broadcasted_iota(jnp.int32, (nrows, LANE), 0) & (nt - 1)
        tk = lax.broadcasted_iota(jnp.int32, (nrows, LANE), 1)
        tn = jnp.where(tk <= tq, _nt_dot(qs_ref[...], kn) * c - head_rows(fnew) * LOG2E, NEG)
        _online(tn, rows, m_ref, l_ref, al_ref, pn_ref)
        _pv(nrows, pn_ref, al_ref, acc_ref, vn[:, 0:LANE], vn[:, LANE:2 * LANE])
        o_ref[0] = _fox_finish(nt, l_ref, acc_ref)


def _sample_attn(kind, q, kn, vn, kc, vc, extra):
    nb, nt, _ = q.shape
    past = kc.shape[1]
    tkc = min(2048, past)
    assert past % tkc == 0 and nt & (nt - 1) == 0 and nt <= LANE
    kn = jnp.pad(kn, ((0, 0), (0, LANE - nt), (0, 0)))
    vn = jnp.pad(vn, ((0, 0), (0, LANE - nt), (0, 0)))
    new = pl.BlockSpec((1, nt, BRANCH_W), lambda b, i: (b, 0, 0))
    newkv = pl.BlockSpec((1, LANE, BRANCH_W), lambda b, i: (b, 0, 0))
    cache = pl.BlockSpec((1, tkc, BRANCH_W), lambda b, i: (b, i, 0))
    if kind == "diff":
        consts, lqk, sub = extra
        nj = 2 * DIFF_HEADS
        kern = functools.partial(_diff_dec_kernel, nt, tkc, past)
        ins = [consts, lqk, sub, q, kn, vn, kc, vc]
        in_specs = [_whole(pltpu.SMEM), _whole(), _whole(), new, newkv, newkv, cache, cache]
    else:
        lfn, fc = extra
        nj = FOX_HEADS
        kern = functools.partial(_fox_dec_kernel, nt, tkc)
        ins = [q, kn, vn, jnp.pad(lfn, ((0, 0), (0, 0), (0, LANE - nt))), kc, vc, fc]
        in_specs = [new, newkv, newkv, pl.BlockSpec((1, 8, LANE), lambda b, i: (b, 0, 0)), cache, cache,
                    pl.BlockSpec((1, 8, tkc), lambda b, i: (b, 0, i))]
    nrows = nj * nt
    return pl.pallas_call(
        kern,
        grid=(nb, past // tkc),
        in_specs=in_specs,
        out_specs=new,
        out_shape=jax.ShapeDtypeStruct((nb, nt, BRANCH_W), BF16),
        scratch_shapes=_attn_scratch(nrows, tkc) + [pltpu.VMEM((nrows, LANE), BF16)],
        compiler_params=_params(("arbitrary", "arbitrary")),
        name=kind + "_attn_sample",
    )(*ins)


def _merge_kernel(x_ref, oa_ref, ob_ref, oc_ref, od_ref, g_ref, wg_ref, bg_ref, wb_ref, wo_ref, y_ref):
    x = x_ref[0]
    g = g_ref[...]
    bg = bg_ref[...]
    h = _rms(x, g[0:1]).astype(BF16)
    merged = None
    for i, o_ref in enumerate((oa_ref, ob_ref, oc_ref, od_ref)):
        gate = jax.nn.sigmoid(_dot(h, wg_ref[i]) + bg[i:i + 1])
        term = gate * _dot(o_ref[0], wb_ref[i])
        merged = term if merged is None else merged + term
    y_ref[0] = x + _rms(_dot(merged.astype(BF16), wo_ref[...]), g[1:2])


def _ffn_kernel(x_ref, g_ref, wi_ref, wo_ref, y_ref):
    x = x_ref[0]
    g = g_ref[...]
    h = _rms(x, g[2:3]).astype(BF16)
    a = jax.nn.silu(_dot(h, wi_ref[:, 0:D_FF])) * _dot(h, wi_ref[:, D_FF:2 * D_FF])
    y_ref[0] = x + _rms(_dot(a.astype(BF16), wo_ref[...]), g[3:4])


def _merge_ffn(x, branches, g, wg, bg, wb, wo, wi, wf):
    nb, seq, _ = x.shape
    tm, _ = _tiles(seq)
    xspec = pl.BlockSpec((1, tm, D_MODEL), lambda b, t: (b, t, 0))
    bspec = pl.BlockSpec((1, tm, BRANCH_W), lambda b, t: (b, t, 0))
    xs = jax.ShapeDtypeStruct(x.shape, F32)
    x1 = pl.pallas_call(
        _merge_kernel,
        grid=(nb, seq // tm),
        in_specs=[xspec, bspec, bspec, bspec, bspec] + [_whole()] * 5,
        out_specs=xspec,
        out_shape=xs,
        compiler_params=_params(("arbitrary", "arbitrary")),
        name="merge",
    )(x, *branches, g, wg, bg, wb, wo)
    return pl.pallas_call(
        _ffn_kernel,
        grid=(nb, seq // tm),
        in_specs=[xspec] + [_whole()] * 3,
        out_specs=xspec,
        out_shape=xs,
        compiler_params=_params(("arbitrary", "arbitrary")),
        name="ffn",
    )(x1, g, wi, wf)


def _block_diag(w):
    n, c, _ = w.shape
    eye = jnp.eye(n, dtype=w.dtype)
    return jnp.einsum("gce,gh->gche", w, eye).reshape(n * c, n * c)


def _to_tmajor(a):
    a = jnp.swapaxes(a, 0, 1)
    return a.reshape((1, a.shape[0] * a.shape[1]) + a.shape[2:])


def _from_tmajor(a, nb):
    return jnp.swapaxes(a.reshape(a.shape[1] // nb, nb, a.shape[2]), 0, 1)


def kernel(x_prompt, x_sample, cache_diff_k, cache_diff_v, cache_fox_k, cache_fox_v, cache_fox_lf, state_conv, state_pool, g_norm, w_in, b_forget, conv_w, lambda_qk, diff_subln, pool_w, pool_scale, w_branch, w_gate, b_gate, w_out, w_ffn_in, w_ffn_out):
    depth = w_in.shape[0]
    nb, seq, _ = x_prompt.shape
    ndb, nt, _ = x_sample.shape
    past = cache_diff_k.shape[2]
    _, tb = _tiles(seq)

    w_pack = jnp.concatenate(
        [w_in[:, :, :N_MAIN], w_in[:, :, N_MAIN + FOX_HEADS:],
         jnp.pad(w_in[:, :, N_MAIN:N_MAIN + FOX_HEADS], ((0, 0), (0, 0), (0, LANE - FOX_HEADS)))], axis=-1).astype(BF16)
    bf_pack = jnp.pad(b_forget, ((0, 0), (0, LANE - FOX_HEADS)))[:, None, :]
    pw_bd = jax.vmap(_block_diag)(pool_w).astype(BF16)
    sub_t = jnp.tile(diff_subln, (1, DIFF_HEADS))[:, None, :]
    ps = pool_scale[:, None, :]
    wg = w_gate.astype(BF16)
    wb = w_branch.astype(BF16)
    wo = w_out.astype(BF16)
    wi = w_ffn_in.astype(BF16)
    wf = w_ffn_out.astype(BF16)

    cdk = cache_diff_k.reshape(depth, ndb, past, BRANCH_W)
    cdv = cache_diff_v.reshape(depth, ndb, past, BRANCH_W)
    cfk = cache_fox_k.reshape(depth, ndb, past, BRANCH_W)
    cfv = cache_fox_v.reshape(depth, ndb, past, BRANCH_W)
    clf = jnp.pad(jnp.swapaxes(cache_fox_lf, 2, 3), ((0, 0), (0, 0), (0, 8 - FOX_HEADS), (0, 0)))

    yp = x_prompt
    ys = _to_tmajor(x_sample)
    p_out = [[] for _ in range(7)]
    s_out = [[] for _ in range(7)]
    for l in range(depth):
        lam_init = 0.8 - 0.6 * math.exp(-0.3 * l)
        consts = jnp.array([lam_init, 1.0 - lam_init], F32)
        g = g_norm[l]

        (oa, od, dq, dk, dv, fq, fk, fv, lf, dkt, fkt, dvb, fvb, lft, nconv, npool) = _inproj(
            yp, g[0:1], w_pack[l], conv_w[l], bf_pack[l], pw_bd[l], ps[l], prompt=True)
        fcum = _cumsum(lft)
        fblk = jnp.swapaxes(fcum.reshape(nb, 8, seq // tb, tb), 1, 2)
        ob = _prompt_attn("diff", dq, dkt, dvb, (consts, lambda_qk[l], sub_t[l]))
        oc = _prompt_attn("fox", fq, fkt, fvb, (fblk,))
        yp = _merge_ffn(yp, (oa, ob, oc, od), g, wg[l], b_gate[l], wb[l], wo[l], wi[l], wf[l])
        for lst, val in zip(p_out, (dk, dv, fk, fv, lf, nconv, npool)):
            lst.append(val)

        zh = _to_tmajor(state_conv[l])[0]
        uh = _to_tmajor(state_pool[l])[0]
        (oa, od, dq, dk, dv, fq, fk, fv, lf, z, u) = _inproj(
            ys, g[0:1], w_pack[l], conv_w[l], bf_pack[l], pw_bd[l], ps[l],
            prompt=False, st=ndb, zh=zh, uh=uh, start_pos=past)
        dq, dk, dv, fq, fk, fv, lf, z, u = (_from_tmajor(a, ndb) for a in (dq, dk, dv, fq, fk, fv, lf, z, u))
        lfn = jnp.pad(jnp.swapaxes(lf, 1, 2), ((0, 0), (0, 8 - FOX_HEADS), (0, 0)))
        fcache = _cumsum(clf[l])
        ob = _sample_attn("diff", dq, dk, dv, cdk[l], cdv[l], (consts, lambda_qk[l], sub_t[l]))
        oc = _sample_attn("fox", fq, fk, fv, cfk[l], cfv[l], (lfn, fcache))
        ys = _merge_ffn(ys, (oa, _to_tmajor(ob), _to_tmajor(oc), od), g, wg[l], b_gate[l], wb[l], wo[l], wi[l], wf[l])
        ext_z = jnp.concatenate([state_conv[l], z], axis=1)
        ext_u = jnp.concatenate([state_pool[l], u], axis=1)
        for lst, val in zip(s_out, (dk, dv, fk, fv, lf, ext_z[:, -(CONV_K - 1):], ext_u[:, -POOL_HIST:])):
            lst.append(val)

    def pack(lists, b, t):
        dk, dv, fk, fv, lf, cv, pool = [jnp.stack(v) for v in lists]
        return (dk.reshape(depth, b, t, DIFF_HEADS, 2, DIFF_DH), dv.reshape(depth, b, t, DIFF_HEADS, 2 * DIFF_DH),
                fk.reshape(depth, b, t, FOX_HEADS, FOX_DH), fv.reshape(depth, b, t, FOX_HEADS, FOX_DH), lf, cv, pool)

    return (yp, _from_tmajor(ys, ndb)) + pack(p_out, nb, seq) + pack(s_out, ndb, nt)
```
